```python
import math
import jax, jax.numpy as jnp
from jax import lax
import numpy as np

D_MODEL = 4096
BATCH = 4
SEQ = 2048
DEPTH = 2
DEC_BATCH = 8
DEC_SEQ = 2048
PAST_LEN = 128

GRID_W = 64
C_A = D_MODEL // 4
G_A = 4
CG_A = C_A // G_A
C_B = D_MODEL // 2
HD_B = 64
H_B = C_B // HD_B
LORA_W = 64
LORA_A = 64
LORA_G = 128
DECAY_SCALE = math.exp(-0.5)
GN_EPS = 64e-5
SHIFT_W = 3 * C_B + LORA_W + LORA_A
C_C = D_MODEL // 4
HD_C = 64
H_C = C_C // HD_C
NA_ROWS = 8
WIN_W = 16
N_KEYS = 128
N_EXPERTS = N_KEYS * N_KEYS
H_P = 8
DK = 256
DK_HALF = DK // 2
TOPK = 16
PEER_BLOCK = 128
N_BRANCH = 3
RMS_EPS = 1e-6

OFF_R = C_A
OFF_K = OFF_R + C_B
OFF_V = OFF_K + C_B
OFF_WD = OFF_V + C_B
OFF_AD = OFF_WD + 2 * LORA_W
OFF_GD = OFF_AD + 2 * LORA_A
OFF_Q = OFF_GD + LORA_G
OFF_KC = OFF_Q + C_C
OFF_VC = OFF_KC + C_C
OFF_GATE = OFF_VC + C_C
W_IN = OFF_GATE + N_BRANCH * D_MODEL
SPLITS = (OFF_R, OFF_K, OFF_V, OFF_WD, OFF_AD, OFF_GD, OFF_Q, OFF_KC, OFF_VC, OFF_GATE)

kernel_name = 'hybrid_fnet_rwkv7_natten_peer_encoder'


def rmsnorm(x, g):
    xf = x.astype(jnp.float32)
    y = xf * lax.rsqrt(jnp.mean(xf * xf, axis=-1, keepdims=True) + RMS_EPS)
    return y.astype(x.dtype) * g


def fourier_mix(za):
    b, s, _ = za.shape
    zz = za.astype(jnp.float32).reshape(b, s, G_A, CG_A)
    out = jnp.fft.fft2(zz, axes=(1, 3), norm='ortho').real
    return out.reshape(b, s, C_A).astype(za.dtype)


def rwkv7_bidir(zr, zk, zv, zwd, zad, zg, mu, w0, w_up, a0, a_up, r_k, k_k, k_a, g_up, lnx_g, lnx_b):
    b, s, _ = zr.shape
    dt = zr.dtype
    f32 = jnp.float32
    fwd = jnp.concatenate([zr, zk, zv, zwd[:, :, 0], zad[:, :, 0]], axis=-1)
    bwd = jnp.concatenate([zr, zk, zv, zwd[:, :, 1], zad[:, :, 1]], axis=-1)[:, ::-1]
    z = jnp.stack([fwd, bwd], axis=0).astype(f32)
    z_prev = jnp.pad(z[:, :, :-1], ((0, 0), (0, 0), (1, 0), (0, 0)))
    z = z + mu.astype(f32)[:, None, None, :] * (z_prev - z)
    r = z[..., :C_B]
    k = z[..., C_B:2 * C_B]
    v = z[..., 2 * C_B:3 * C_B]
    wlo = z[..., 3 * C_B:3 * C_B + LORA_W]
    alo = z[..., 3 * C_B + LORA_W:]
    w = jnp.exp(-DECAY_SCALE * jax.nn.sigmoid(
        w0.astype(f32)[:, None, None] + jnp.einsum('nbsl,nlc->nbsc', jnp.tanh(wlo), w_up.astype(f32))))
    a = jax.nn.sigmoid(a0.astype(f32)[:, None, None] + jnp.einsum('nbsl,nlc->nbsc', alo, a_up.astype(f32)))

    def heads(t):
        return t.reshape(2, b, s, H_B, HD_B)

    kk = heads(k * k_k.astype(f32))
    kk = kk * lax.rsqrt(jnp.maximum(jnp.sum(kk * kk, axis=-1, keepdims=True), 1e-12))
    k = heads(k * (1.0 + (a - 1.0) * k_a.astype(f32)))
    r, w, v, a = heads(r), heads(w), heads(v), heads(a)
    bonus = jnp.sum(r * k * r_k.astype(f32)[:, None, None], axis=-1, keepdims=True) * v
    xs = tuple(jnp.moveaxis(t, 2, 0) for t in (r, w, k, v, kk, kk * a))

    def step(state, inp):
        r_t, w_t, k_t, v_t, kk_t, b_t = inp
        sa = jnp.einsum('nbhvk,nbhk->nbhv', state, -kk_t)
        state = state * w_t[..., None, :] + sa[..., None] * b_t[..., None, :] + v_t[..., None] * k_t[..., None, :]
        return state, jnp.einsum('nbhvk,nbhk->nbhv', state, r_t)

    state0 = jnp.zeros((2, b, H_B, HD_B, HD_B), f32)
    _, o = lax.scan(step, state0, xs)
    o = jnp.moveaxis(o, 0, 2)
    o = o[0] + o[1, :, ::-1]
    bonus = bonus[0] + bonus[1, :, ::-1]
    mean = jnp.mean(o, axis=-1, keepdims=True)
    var = jnp.mean(jnp.square(o - mean), axis=-1, keepdims=True)
    o = ((o - mean) * lax.rsqrt(var + GN_EPS)).reshape(b, s, C_B) * lnx_g.astype(f32) + lnx_b.astype(f32)
    g = jax.nn.sigmoid(zg.astype(f32)) @ g_up.astype(f32)
    return ((o + bonus.reshape(b, s, C_B)) * g).astype(dt)


def neighbourhood_attention(q, k, v, rpb):
    b, s, _ = q.shape
    rows = s // GRID_W
    kr = min(NA_ROWS, rows)
    n_nb = kr * WIN_W
    scale = HD_C ** -0.5
    qg = q.reshape(b, rows, GRID_W, H_C, HD_C)
    kg = k.reshape(b, rows, GRID_W, H_C, HD_C)
    vg = v.reshape(b, rows, GRID_W, H_C, HD_C)
    cols = np.arange(GRID_W)
    cstart = np.clip(cols - WIN_W // 2, 0, GRID_W - WIN_W)
    cidx_np = cstart[:, None] + np.arange(WIN_W)[None, :]
    dc = jnp.asarray(cidx_np - cols[:, None] + WIN_W - 1)
    cidx = jnp.asarray(cidx_np)

    def one_row(args):
        rw, q_row = args
        rs = jnp.clip(rw - kr // 2, 0, rows - kr)

        def gather(t):
            blk = lax.dynamic_slice_in_dim(t, rs, kr, axis=1)
            nb = blk[:, :, cidx]
            return nb.transpose(0, 2, 1, 3, 4, 5).reshape(b, GRID_W, n_nb, H_C, HD_C)

        k_nb = gather(kg)
        v_nb = gather(vg)
        dr = rs + jnp.arange(kr) - rw + NA_ROWS - 1
        bias = rpb[:, dr[None, :, None], dc[:, None, :]].reshape(H_C, GRID_W, n_nb)
        sc = jnp.einsum('bjhd,bjnhd->bhjn', q_row * scale, k_nb).astype(jnp.float32)
        sc = sc + bias.astype(jnp.float32)[None]
        p = jax.nn.softmax(sc, axis=-1).astype(v.dtype)
        return jnp.einsum('bhjn,bjnhd->bjhd', p, v_nb)

    out = lax.map(one_row, (jnp.arange(rows), jnp.moveaxis(qg, 1, 0)))
    return jnp.moveaxis(out, 0, 1).reshape(b, s, C_C)


def peer_ffn(xn, wq, subkeys, u_tab, v_tab):
    b, s, d = xn.shape
    t = b * s
    xt = xn.reshape(t, d)
    q = (xt @ wq).reshape(t, H_P, 2, DK_HALF)
    sc = jnp.einsum('thpd,hpnd->thpn', q, subkeys).astype(jnp.float32)
    sv, si = lax.top_k(sc, TOPK)
    cand = (sv[:, :, 0, :, None] + sv[:, :, 1, None, :]).reshape(t, H_P, TOPK * TOPK)
    cv, ci = lax.top_k(cand, TOPK)
    i1 = jnp.take_along_axis(si[:, :, 0], ci // TOPK, axis=-1)
    i2 = jnp.take_along_axis(si[:, :, 1], ci % TOPK, axis=-1)
    eid = (i1 * N_KEYS + i2).reshape(t, H_P * TOPK)
    gate = jax.nn.softmax(cv, axis=-1).reshape(t, H_P * TOPK).astype(xn.dtype)
    nblk = t // PEER_BLOCK

    def one_block(args):
        xb, eb, gb = args
        act = jax.nn.gelu(jnp.einsum('tnd,td->tn', u_tab[eb], xb))
        return jnp.einsum('tn,tnd->td', gb * act, v_tab[eb])

    y = lax.map(one_block, (xt.reshape(nblk, PEER_BLOCK, d),
                            eid.reshape(nblk, PEER_BLOCK, H_P * TOPK),
                            gate.reshape(nblk, PEER_BLOCK, H_P * TOPK)))
    return y.reshape(b, s, d)


def encoder_trunk(x, norm1_g, w_in, rwkv_mu, rwkv_w0, rwkv_w_up, rwkv_a0, rwkv_a_up, rwkv_r_k,
                  rwkv_k_k, rwkv_k_a, rwkv_g_up, rwkv_lnx_g, rwkv_lnx_b, na_rpb,
                  w_branch_a, w_branch_b, w_branch_c, w_out, norm2_g,
                  peer_wq, peer_subkeys, peer_u, peer_v, final_g):
    b, s, _ = x.shape
    for i in range(DEPTH):
        h = rmsnorm(x, norm1_g[i])
        z = h @ w_in[i]
        za, zr, zk, zv, zwd, zad, zg, zq, zkc, zvc, zgate = jnp.split(z, SPLITS, axis=-1)
        y_a = fourier_mix(za)
        y_b = rwkv7_bidir(zr, zk, zv, zwd.reshape(b, s, 2, LORA_W), zad.reshape(b, s, 2, LORA_A), zg,
                          rwkv_mu[i], rwkv_w0[i], rwkv_w_up[i], rwkv_a0[i], rwkv_a_up[i], rwkv_r_k[i],
                          rwkv_k_k[i], rwkv_k_a[i], rwkv_g_up[i], rwkv_lnx_g[i], rwkv_lnx_b[i])
        y_c = neighbourhood_attention(zq, zkc, zvc, na_rpb[i])
        gates = jax.nn.sigmoid(zgate.reshape(b, s, N_BRANCH, D_MODEL))
        merged = (gates[:, :, 0] * (y_a @ w_branch_a[i])
                  + gates[:, :, 1] * (y_b @ w_branch_b[i])
                  + gates[:, :, 2] * (y_c @ w_branch_c[i]))
        x = x + merged @ w_out[i]
        x = x + peer_ffn(rmsnorm(x, norm2_g[i]), peer_wq[i], peer_subkeys[i], peer_u[i], peer_v[i])
    return rmsnorm(x, final_g)


def setup_inputs(seed: int = 0) -> dict:
    key = jax.random.key(seed)
    ks = jax.random.split(key, 26)

    def nrm(k, shape, scale):
        return jax.random.normal(k, shape, jnp.float32) * scale

    return {
        'x_prompt': nrm(ks[0], (BATCH, SEQ, D_MODEL), 1.0),
        'x_sample': nrm(ks[1], (DEC_BATCH, DEC_SEQ, D_MODEL), 1.0),
        'norm1_g': 1.0 + nrm(ks[2], (DEPTH, D_MODEL), 0.02),
        'w_in': nrm(ks[3], (DEPTH, D_MODEL, W_IN), D_MODEL ** -0.5),
        'rwkv_mu': jax.random.uniform(ks[4], (DEPTH, 2, SHIFT_W), jnp.float32),
        'rwkv_w0': nrm(ks[5], (DEPTH, 2, C_B), 0.5),
        'rwkv_w_up': nrm(ks[6], (DEPTH, 2, LORA_W, C_B), LORA_W ** -0.5),
        'rwkv_a0': nrm(ks[7], (DEPTH, 2, C_B), 0.1),
        'rwkv_a_up': nrm(ks[8], (DEPTH, 2, LORA_A, C_B), LORA_A ** -0.5),
        'rwkv_r_k': nrm(ks[9], (DEPTH, 2, H_B, HD_B), 0.1),
        'rwkv_k_k': 0.85 + nrm(ks[10], (DEPTH, C_B), 0.02),
        'rwkv_k_a': 1.0 + nrm(ks[11], (DEPTH, C_B), 0.02),
        'rwkv_g_up': nrm(ks[12], (DEPTH, LORA_G, C_B), LORA_G ** -0.5),
        'rwkv_lnx_g': 1.0 + nrm(ks[13], (DEPTH, C_B), 0.02),
        'rwkv_lnx_b': nrm(ks[14], (DEPTH, C_B), 0.02),
        'na_rpb': nrm(ks[15], (DEPTH, H_C, 2 * NA_ROWS - 1, 2 * WIN_W - 1), 0.1),
        'w_branch_a': nrm(ks[16], (DEPTH, C_A, D_MODEL), C_A ** -0.5),
        'w_branch_b': nrm(ks[17], (DEPTH, C_B, D_MODEL), C_B ** -0.5),
        'w_branch_c': nrm(ks[18], (DEPTH, C_C, D_MODEL), C_C ** -0.5),
        'w_out': nrm(ks[19], (DEPTH, D_MODEL, D_MODEL), D_MODEL ** -0.5),
        'norm2_g': 1.0 + nrm(ks[20], (DEPTH, D_MODEL), 0.02),
        'peer_wq': nrm(ks[21], (DEPTH, D_MODEL, H_P * DK), D_MODEL ** -0.5),
        'peer_subkeys': nrm(ks[22], (DEPTH, H_P, 2, N_KEYS, DK_HALF), DK_HALF ** -0.5),
        'peer_u': nrm(ks[23], (DEPTH, N_EXPERTS, D_MODEL), D_MODEL ** -0.5),
        'peer_v': nrm(ks[24], (DEPTH, N_EXPERTS, D_MODEL), H_P ** -0.5),
        'final_g': 1.0 + nrm(ks[25], (D_MODEL,), 0.02),
    }


def reference(x_prompt, x_sample, norm1_g, w_in, rwkv_mu, rwkv_w0, rwkv_w_up, rwkv_a0, rwkv_a_up,
              rwkv_r_k, rwkv_k_k, rwkv_k_a, rwkv_g_up, rwkv_lnx_g, rwkv_lnx_b, na_rpb,
              w_branch_a, w_branch_b, w_branch_c, w_out, norm2_g,
              peer_wq, peer_subkeys, peer_u, peer_v, final_g):
    y_prompt = encoder_trunk(x_prompt, norm1_g, w_in, rwkv_mu, rwkv_w0, rwkv_w_up, rwkv_a0, rwkv_a_up,
                             rwkv_r_k, rwkv_k_k, rwkv_k_a, rwkv_g_up, rwkv_lnx_g, rwkv_lnx_b, na_rpb,
                             w_branch_a, w_branch_b, w_branch_c, w_out, norm2_g,
                             peer_wq, peer_subkeys, peer_u, peer_v, final_g)
    y_sample = encoder_trunk(x_sample, norm1_g, w_in, rwkv_mu, rwkv_w0, rwkv_w_up, rwkv_a0, rwkv_a_up,
                             rwkv_r_k, rwkv_k_k, rwkv_k_a, rwkv_g_up, rwkv_lnx_g, rwkv_lnx_b, na_rpb,
                             w_branch_a, w_branch_b, w_branch_c, w_out, norm2_g,
                             peer_wq, peer_subkeys, peer_u, peer_v, final_g)
    return (y_prompt, y_sample)
```

```python
import functools
import math

import numpy as np
import jax
import jax.numpy as jnp
from jax import lax
from jax.experimental import pallas as pl
from jax.experimental.pallas import tpu as pltpu

F32 = jnp.float32
BF16 = jnp.bfloat16

V7X_VMEM_BYTES = 64 * 1024 * 1024
VMEM_LIMIT = V7X_VMEM_BYTES - 8 * 1024 * 1024
LANES = 128

RMS_EPS = 1e-6
GN_EPS = 64e-5
DECAY_SCALE = math.exp(-0.5)
GRID_W = 64
NA_ROWS = 8
WIN_W = 16
HD = 64
G_A = 4
N_KEYS = 128
TOPK = 16
H_P = 8
SCAN_CHUNK = 64
NEG_INF = float("-inf")


def _params(sem):
    return pltpu.CompilerParams(dimension_semantics=sem, vmem_limit_bytes=VMEM_LIMIT)


def _tile(n, pref):
    if n <= pref:
        return n
    t = pref
    while n % t:
        t //= 2
    return t


def _dot(a, b):
    return jnp.dot(a, b, preferred_element_type=F32)


def _dot_nt(a, b):
    return lax.dot_general(a, b, (((1,), (1,)), ((), ())), preferred_element_type=F32)


def _sigmoid(x):
    return 1.0 / (1.0 + jnp.exp(-x))


def _rmsnorm_kernel(x_ref, g_ref, o_ref):
    x = x_ref[...]
    y = x * lax.rsqrt(jnp.mean(x * x, axis=-1, keepdims=True) + RMS_EPS)
    o_ref[...] = (y * g_ref[...]).astype(o_ref.dtype)


def rmsnorm(x, g, out_dtype):
    t, d = x.shape
    tm = _tile(t, 512)
    return pl.pallas_call(
        _rmsnorm_kernel,
        grid=(t // tm,),
        in_specs=[pl.BlockSpec((tm, d), lambda i: (i, 0)), pl.BlockSpec((1, d), lambda i: (0, 0))],
        out_specs=pl.BlockSpec((tm, d), lambda i: (i, 0)),
        out_shape=jax.ShapeDtypeStruct((t, d), out_dtype),
        compiler_params=_params(("parallel",)),
        name="rmsnorm",
    )(x, g.reshape(1, d))


def _mm_kernel(*refs, epilogue, n_extra):
    a_ref, b_ref = refs[0], refs[1]
    extras = refs[2:2 + n_extra]
    o_ref = refs[2 + n_extra]
    acc = _dot(a_ref[...], b_ref[...])
    o_ref[...] = epilogue(acc, *[e[...] for e in extras]).astype(o_ref.dtype)


def matmul(a, b, out_dtype, epilogue=None, tiles=(), rows=(), tm=1024, tn=1024, name="matmul"):
    m, k = a.shape
    n = b.shape[1]
    tm, tn = _tile(m, tm), _tile(n, tn)
    if epilogue is None:
        epilogue = lambda acc: acc
    in_specs = [pl.BlockSpec((tm, k), lambda i, j: (i, 0)), pl.BlockSpec((k, tn), lambda i, j: (0, j))]
    in_specs += [pl.BlockSpec((tm, tn), lambda i, j: (i, j)) for _ in tiles]
    in_specs += [pl.BlockSpec((1, tn), lambda i, j: (0, j)) for _ in rows]
    return pl.pallas_call(
        functools.partial(_mm_kernel, epilogue=epilogue, n_extra=len(tiles) + len(rows)),
        grid=(m // tm, n // tn),
        in_specs=in_specs,
        out_specs=pl.BlockSpec((tm, tn), lambda i, j: (i, j)),
        out_shape=jax.ShapeDtypeStruct((m, n), out_dtype),
        compiler_params=_params(("parallel", "parallel")),
        name=name,
    )(a, b, *tiles, *[r.reshape(1, n) for r in rows])


def _dft_mats(n, scale):
    idx = jnp.arange(n, dtype=jnp.int32)
    kk = (idx[:, None] * idx[None, :]) % n
    ang = kk.astype(F32) * (2.0 * math.pi / n)
    return (jnp.cos(ang) * scale).astype(BF16), (jnp.sin(ang) * scale).astype(BF16)


def _fourier_kernel(x_ref, cs_ref, ss_ref, cc_ref, sc_ref, o_ref, *, groups, cg):
    x = x_ref[0]
    p = _dot(cs_ref[...], x).astype(BF16)
    q = _dot(ss_ref[...], x).astype(BF16)
    for g in range(groups):
        sl = slice(g * cg, (g + 1) * cg)
        o_ref[0, :, sl] = (_dot(p[:, sl], cc_ref[...]) - _dot(q[:, sl], sc_ref[...])).astype(o_ref.dtype)


def fourier_mix(za):
    b, s, c = za.shape
    cg = c // G_A
    cs, ss = _dft_mats(s, s ** -0.5)
    cc, sc = _dft_mats(cg, cg ** -0.5)
    tr = _tile(s, 512)
    return pl.pallas_call(
        functools.partial(_fourier_kernel, groups=G_A, cg=cg),
        grid=(b, s // tr),
        in_specs=[
            pl.BlockSpec((1, s, c), lambda i, j: (i, 0, 0)),
            pl.BlockSpec((tr, s), lambda i, j: (j, 0)),
            pl.BlockSpec((tr, s), lambda i, j: (j, 0)),
            pl.BlockSpec((cg, cg), lambda i, j: (0, 0)),
            pl.BlockSpec((cg, cg), lambda i, j: (0, 0)),
        ],
        out_specs=pl.BlockSpec((1, tr, c), lambda i, j: (i, j, 0)),
        out_shape=jax.ShapeDtypeStruct((b, s, c), BF16),
        compiler_params=_params(("parallel", "parallel")),
        name="fourier_mix",
    )(za, cs, ss, cc, sc)


def _scan_kernel(r_ref, lw_ref, k_ref, v_ref, kk_ref, a_ref, o_ref, s_scr, *, pairs, chunk):
    c = chunk

    @pl.when(pl.program_id(2) == 0)
    def _():
        s_scr[...] = jnp.zeros_like(s_scr)

    row = lax.broadcasted_iota(jnp.int32, (c, c), 0)
    col = lax.broadcasted_iota(jnp.int32, (c, c), 1)
    tri_incl = row >= col
    tri_strict = row > col
    tri_bf = jnp.where(tri_incl, 1.0, 0.0).astype(BF16)
    eye = jnp.where(row == col, 1.0, 0.0).astype(F32)
    lane = lax.broadcasted_iota(jnp.int32, (c, LANES), 1)
    first = lane < HD
    head_mask = [jnp.where(first, 1.0, 0.0).astype(F32), jnp.where(first, 0.0, 1.0).astype(F32)]
    br = lax.broadcasted_iota(jnp.int32, (LANES, LANES), 0) < HD
    bc = lax.broadcasted_iota(jnp.int32, (LANES, LANES), 1) < HD
    blockdiag = br == bc

    for p in range(pairs):
        sl = slice(p * LANES, (p + 1) * LANES)
        lw = lw_ref[0, :, sl]
        lw_hi = lw.astype(BF16)
        lw_lo = (lw - lw_hi.astype(F32)).astype(BF16)
        cum = _dot(tri_bf, lw_hi) + _dot(tri_bf, lw_lo)
        g_incl = jnp.exp(cum)
        g_excl = jnp.exp(cum - lw)
        g_inv = jnp.exp(-cum)
        g_end = g_incl[c - 1:c, :]
        r = r_ref[0, :, sl]
        k = k_ref[0, :, sl]
        v = v_ref[0, :, sl]
        kk = kk_ref[0, :, sl]
        b = kk * a_ref[0, :, sl]
        rt = r * g_incl
        at = -kk * g_excl
        bt = (b * g_inv).astype(BF16)
        kt = (k * g_inv).astype(BF16)
        s0 = s_scr[p]
        s0_bf = s0.astype(BF16)
        v_bf = v.astype(BF16)

        a_ak, a_rb, a_rk, tinv = [], [], [], []
        for h in range(2):
            at_h = (at * head_mask[h]).astype(BF16)
            rt_h = (rt * head_mask[h]).astype(BF16)
            a_ab = jnp.where(tri_strict, _dot_nt(at_h, bt), 0.0)
            a_ak.append(jnp.where(tri_strict, _dot_nt(at_h, kt), 0.0).astype(BF16))
            a_rb.append(jnp.where(tri_incl, _dot_nt(rt_h, bt), 0.0).astype(BF16))
            a_rk.append(jnp.where(tri_incl, _dot_nt(rt_h, kt), 0.0).astype(BF16))
            t_h = eye + a_ab
            pw = a_ab
            steps = 1
            while steps * 2 < c:
                pw_bf = pw.astype(BF16)
                pw = _dot(pw_bf, pw_bf)
                t_h = t_h + _dot(t_h.astype(BF16), pw.astype(BF16))
                steps *= 2
            tinv.append(t_h.astype(BF16))

        def per_head(f):
            return jnp.where(first, f(0), f(1))

        x = _dot_nt(at.astype(BF16), s0_bf) + per_head(lambda h: _dot(a_ak[h], v_bf))
        x_bf = x.astype(BF16)
        u = per_head(lambda h: _dot(tinv[h], x_bf))
        u_bf = u.astype(BF16)
        o = _dot_nt(rt.astype(BF16), s0_bf) + per_head(lambda h: _dot(a_rb[h], u_bf) + _dot(a_rk[h], v_bf))
        o_ref[0, :, sl] = o
        uv = jnp.concatenate([u, v], axis=0)
        bk = jnp.concatenate([b * g_inv * g_end, k * g_inv * g_end], axis=0)
        upd = _dot(uv.T.astype(BF16), bk.astype(BF16))
        s_scr[p] = s0 * g_end + jnp.where(blockdiag, upd, 0.0)


def rwkv7_scan(r, lw, k, v, kk, a):
    n, s, c = r.shape
    chunk = _tile(s, SCAN_CHUNK)
    pairs = _tile(c // LANES, 4)
    wb = pairs * LANES
    spec = pl.BlockSpec((1, chunk, wb), lambda i, j, t: (i, t, j))
    return pl.pallas_call(
        functools.partial(_scan_kernel, pairs=pairs, chunk=chunk),
        grid=(n, c // wb, s // chunk),
        in_specs=[spec] * 6,
        out_specs=spec,
        out_shape=jax.ShapeDtypeStruct((n, s, c), F32),
        scratch_shapes=[pltpu.VMEM((pairs, LANES, LANES), F32)],
        compiler_params=_params(("parallel", "parallel", "arbitrary")),
        name="rwkv7_scan",
    )(r, lw, k, v, kk, a)


def rwkv7_bidir(zr, zk, zv, zlo, mu, w0, w_up, a0, a_up, r_k, k_k, k_a, g_up, lnx_g, lnx_b):
    bsz, s, c = zr.shape
    lw_w = w_up.shape[1]
    la_w = a_up.shape[1]
    nh = c // HD
    zwd = zlo[..., :2 * lw_w].reshape(bsz, s, 2, lw_w)
    zad = zlo[..., 2 * lw_w:2 * lw_w + 2 * la_w].reshape(bsz, s, 2, la_w)
    zg = zlo[..., 2 * lw_w + 2 * la_w:]
    fwd = jnp.concatenate([zr, zk, zv, zwd[:, :, 0], zad[:, :, 0]], axis=-1)
    bwd = jnp.concatenate([zr, zk, zv, zwd[:, :, 1], zad[:, :, 1]], axis=-1)[:, ::-1]
    z = jnp.stack([fwd, bwd], axis=0)
    z_prev = jnp.pad(z[:, :, :-1], ((0, 0), (0, 0), (1, 0), (0, 0)))
    z = z + mu[:, None, None, :] * (z_prev - z)
    r = z[..., :c]
    k = z[..., c:2 * c]
    v = z[..., 2 * c:3 * c]
    wlo = z[..., 3 * c:3 * c + lw_w]
    alo = z[..., 3 * c + lw_w:]
    lw, a = [], []
    for n in range(2):
        lw.append(matmul(jnp.tanh(wlo[n]).reshape(bsz * s, lw_w).astype(BF16), w_up[n].astype(BF16), F32,
                         epilogue=lambda acc, w0r: -DECAY_SCALE * _sigmoid(acc + w0r), rows=(w0[n],),
                         name="rwkv_decay"))
        a.append(matmul(alo[n].reshape(bsz * s, la_w).astype(BF16), a_up[n].astype(BF16), F32,
                        epilogue=lambda acc, a0r: _sigmoid(acc + a0r), rows=(a0[n],), name="rwkv_iclr"))
    lw = jnp.stack(lw).reshape(2, bsz, s, c)
    a = jnp.stack(a).reshape(2, bsz, s, c)

    def heads(t):
        return t.reshape(2, bsz, s, nh, HD)

    kk = heads(k * k_k)
    kk = (kk * lax.rsqrt(jnp.maximum(jnp.sum(kk * kk, axis=-1, keepdims=True), 1e-12))).reshape(2, bsz, s, c)
    k = k * (1.0 + (a - 1.0) * k_a)
    bonus = (jnp.sum(heads(r * k) * r_k[:, None, None], axis=-1, keepdims=True) * heads(v)).reshape(2, bsz, s, c)

    flat = lambda t: t.reshape(2 * bsz, s, c)
    o = rwkv7_scan(flat(r), flat(lw), flat(k), flat(v), flat(kk), flat(a)).reshape(2, bsz, s, c)
    o = heads(o)
    o = o[0] + o[1, :, ::-1]
    bonus = bonus[0] + bonus[1, :, ::-1]
    mean = jnp.mean(o, axis=-1, keepdims=True)
    var = jnp.mean(jnp.square(o - mean), axis=-1, keepdims=True)
    o = ((o - mean) * lax.rsqrt(var + GN_EPS)).reshape(bsz, s, c) * lnx_g + lnx_b
    pre = (o + bonus).reshape(bsz * s, c)
    return matmul(_sigmoid(zg).reshape(bsz * s, -1).astype(BF16), g_up.astype(BF16), BF16,
                  epilogue=lambda acc, pre_t: acc * pre_t, tiles=(pre,), tm=512, name="rwkv_gate")


def _na_bias_table(rpb, kr):
    cols = np.arange(GRID_W)
    cstart = np.clip(cols - WIN_W // 2, 0, GRID_W - WIN_W)
    kc = np.arange(GRID_W)
    inside = (kc[None, :] >= cstart[:, None]) & (kc[None, :] < cstart[:, None] + WIN_W)
    dc = np.clip(kc[None, :] - cols[:, None] + WIN_W - 1, 0, 2 * WIN_W - 2)
    d0 = NA_ROWS - kr + np.arange(kr)
    dr = d0[:, None] + np.arange(kr)[None, :]
    tab = rpb[:, dr[:, :, None, None], dc[None, None]]
    tab = jnp.where(inside[None, None, None], tab, NEG_INF)
    h = rpb.shape[0]
    return tab.transpose(0, 1, 3, 2, 4).reshape(h, kr, GRID_W, kr * GRID_W).astype(F32)


def _na_kernel(q_ref, k_ref, v_ref, bias_ref, o_ref, *, rows, kr):
    lane = lax.broadcasted_iota(jnp.int32, (GRID_W, LANES), 1)
    first = lane < HD
    scale = HD ** -0.5

    def one_row(rw, carry):
        rs = jnp.clip(rw - kr // 2, 0, rows - kr)
        slab = rs - rw + kr - 1
        q = q_ref[0, pl.ds(pl.multiple_of(rw * GRID_W, GRID_W), GRID_W), :]
        kb = k_ref[0, pl.ds(pl.multiple_of(rs * GRID_W, GRID_W), kr * GRID_W), :]
        vb = v_ref[0, pl.ds(pl.multiple_of(rs * GRID_W, GRID_W), kr * GRID_W), :]
        outs = []
        for h in range(2):
            qh = jnp.where(first if h == 0 else jnp.logical_not(first), q, jnp.zeros_like(q))
            sc = _dot_nt(qh, kb) * scale + bias_ref[h, slab]
            m = jnp.max(sc, axis=-1, keepdims=True)
            p = jnp.exp(sc - m)
            l = jnp.sum(p, axis=-1, keepdims=True)
            outs.append(_dot(p.astype(BF16), vb) / l)
        o_ref[0, pl.ds(pl.multiple_of(rw * GRID_W, GRID_W), GRID_W), :] = jnp.where(first, outs[0], outs[1]).astype(o_ref.dtype)
        return carry

    lax.fori_loop(0, rows, one_row, 0)


def neighbourhood_attention(zqkv, rpb):
    b, s, c3 = zqkv.shape
    c = c3 // 3
    rows = s // GRID_W
    kr = min(NA_ROWS, rows)
    npair = c // LANES
    bias = _na_bias_table(rpb, kr)
    return pl.pallas_call(
        functools.partial(_na_kernel, rows=rows, kr=kr),
        grid=(b, npair),
        in_specs=[
            pl.BlockSpec((1, s, LANES), lambda i, j: (i, 0, j)),
            pl.BlockSpec((1, s, LANES), lambda i, j: (i, 0, npair + j)),
            pl.BlockSpec((1, s, LANES), lambda i, j: (i, 0, 2 * npair + j)),
            pl.BlockSpec((2, kr, GRID_W, kr * GRID_W), lambda i, j: (j, 0, 0, 0)),
        ],
        out_specs=pl.BlockSpec((1, s, LANES), lambda i, j: (i, 0, j)),
        out_shape=jax.ShapeDtypeStruct((b, s, c), BF16),
        compiler_params=_params(("parallel", "parallel")),
        name="neighbourhood_attention",
    )(zqkv, zqkv, zqkv, bias)


def _topk_rows(x, n):
    vals = []
    for _ in range(n):
        m = jnp.max(x, axis=0, keepdims=True)
        vals.append(m)
        x = jnp.where(x >= m, NEG_INF, x)
    return vals


def _route_kernel(q_ref, keys_ref, s1_ref, s2_ref, tau_ref, c_ref):
    tm = q_ref.shape[0]
    q = q_ref[...]
    s1 = _dot_nt(keys_ref[0, 0], q[:, :N_KEYS])
    s2 = _dot_nt(keys_ref[0, 1], q[:, N_KEYS:])
    s1_ref[0] = s1
    s2_ref[0] = s2
    top1 = _topk_rows(s1, TOPK)
    top2 = _topk_rows(s2, TOPK)
    rid = lax.broadcasted_iota(jnp.int32, (TOPK, tm), 0)
    b16 = jnp.full((TOPK, tm), NEG_INF, F32)
    for j in range(TOPK):
        b16 = jnp.where(rid == j, top2[j], b16)
    cand = [top1[i] + b16 for i in range(TOPK)]
    best = []
    for _ in range(TOPK):
        m = cand[0]
        for cnd in cand[1:]:
            m = jnp.maximum(m, cnd)
        m = jnp.max(m, axis=0, keepdims=True)
        best.append(m)
        cand = [jnp.where(cnd >= m, NEG_INF, cnd) for cnd in cand]
    z = jnp.zeros_like(best[0])
    for bv in best:
        z = z + jnp.exp(bv - best[0])
    tau_ref[0] = best[-1]
    c_ref[0] = best[0] + jnp.log(z)


def peer_route(q, subkeys):
    t = q.shape[0]
    tm = _tile(t, 512)
    hp = subkeys.shape[0]
    score = jax.ShapeDtypeStruct((hp, N_KEYS, t), F32)
    scal = jax.ShapeDtypeStruct((hp, 1, t), F32)
    sspec = pl.BlockSpec((1, N_KEYS, tm), lambda i, h: (h, 0, i))
    cspec = pl.BlockSpec((1, 1, tm), lambda i, h: (h, 0, i))
    return pl.pallas_call(
        _route_kernel,
        grid=(t // tm, hp),
        in_specs=[pl.BlockSpec((tm, 2 * N_KEYS), lambda i, h: (i, h)),
                  pl.BlockSpec((1, 2, N_KEYS, subkeys.shape[-1]), lambda i, h: (h, 0, 0, 0))],
        out_specs=[sspec, sspec, cspec, cspec],
        out_shape=[score, score, scal, scal],
        compiler_params=_params(("parallel", "parallel")),
        name="peer_route",
    )(q, subkeys)


def _gelu_tanh(x):
    return 0.5 * x * (1.0 + jnp.tanh(math.sqrt(2.0 / math.pi) * (x + 0.044715 * (x * x * x))))


def _peer_kernel(u_ref, xt_ref, vt_ref, s1_ref, s2_ref, tau_ref, c_ref, o_ref, wa_scr, *, nb):
    j = pl.program_id(1)

    @pl.when(j == 0)
    def _():
        o_ref[...] = jnp.zeros_like(o_ref)

    act = _gelu_tanh(_dot(u_ref[...], xt_ref[...]))
    heads = s1_ref.shape[0]
    for ii in range(nb):
        i1 = j * nb + ii
        gate = None
        for h in range(heads):
            sm = s2_ref[h] + s1_ref[h, pl.ds(i1, 1), :]
            w = jnp.where(sm >= tau_ref[h], jnp.exp(sm - c_ref[h]), 0.0)
            gate = w if gate is None else gate + w
        rows = slice(ii * N_KEYS, (ii + 1) * N_KEYS)
        wa_scr[rows, :] = (gate * act[rows, :]).astype(BF16)
    o_ref[...] += _dot(vt_ref[...], wa_scr[...])


def peer_mix(u, xt, vt, s1, s2, tau, c):
    e, d = u.shape
    t = xt.shape[1]
    hp = s1.shape[0]
    tm = _tile(t, 512)
    nb = 4
    te = nb * N_KEYS
    return pl.pallas_call(
        functools.partial(_peer_kernel, nb=nb),
        grid=(t // tm, e // te),
        in_specs=[
            pl.BlockSpec((te, d), lambda i, j: (j, 0)),
            pl.BlockSpec((d, tm), lambda i, j: (0, i)),
            pl.BlockSpec((d, te), lambda i, j: (0, j)),
            pl.BlockSpec((hp, N_KEYS, tm), lambda i, j: (0, 0, i)),
            pl.BlockSpec((hp, N_KEYS, tm), lambda i, j: (0, 0, i)),
            pl.BlockSpec((hp, 1, tm), lambda i, j: (0, 0, i)),
            pl.BlockSpec((hp, 1, tm), lambda i, j: (0, 0, i)),
        ],
        out_specs=pl.BlockSpec((d, tm), lambda i, j: (0, i)),
        out_shape=jax.ShapeDtypeStruct((d, t), F32),
        scratch_shapes=[pltpu.VMEM((te, tm), BF16)],
        compiler_params=_params(("parallel", "arbitrary")),
        name="peer_mix",
    )(u, xt, vt, s1, s2, tau, c)


def peer_ffn(xn, wq, subkeys, u_tab, v_tab):
    q = matmul(xn, wq.astype(BF16), BF16, name="peer_query")
    s1, s2, tau, c = peer_route(q, subkeys.astype(BF16))
    yt = peer_mix(u_tab.astype(BF16), xn.T, v_tab.T.astype(BF16), s1, s2, tau, c)
    return yt.T


def encoder_trunk(x, p):
    b, s, d = x.shape
    t = b * s
    depth = p["w_in"].shape[0]
    c_a = p["w_branch_a"].shape[1]
    c_b = p["w_branch_b"].shape[1]
    c_c = p["w_branch_c"].shape[1]
    n_lo = 2 * p["rwkv_w_up"].shape[2] + 2 * p["rwkv_a_up"].shape[2] + p["rwkv_g_up"].shape[1]
    off_r = c_a
    off_lo = off_r + 3 * c_b
    off_q = off_lo + n_lo
    off_gate = off_q + 3 * c_c
    x = x.reshape(t, d)
    for i in range(depth):
        w_in = p["w_in"][i]
        h = rmsnorm(x, p["norm1_g"][i], BF16)
        za = matmul(h, w_in[:, :off_r].astype(BF16), BF16, name="in_proj_a")
        zrkv = matmul(h, w_in[:, off_r:off_lo].astype(BF16), F32, name="in_proj_b")
        zlo = matmul(h, w_in[:, off_lo:off_q].astype(BF16), F32, name="in_proj_lo")
        zqkv = matmul(h, w_in[:, off_q:off_gate].astype(BF16), BF16, name="in_proj_c")
        gates = matmul(h, w_in[:, off_gate:].astype(BF16), BF16, epilogue=_sigmoid, name="in_proj_gate")

        y_a = fourier_mix(za.reshape(b, s, c_a)).reshape(t, c_a)
        zrkv = zrkv.reshape(b, s, 3 * c_b)
        y_b = rwkv7_bidir(zrkv[..., :c_b], zrkv[..., c_b:2 * c_b], zrkv[..., 2 * c_b:], zlo.reshape(b, s, n_lo),
                          p["rwkv_mu"][i], p["rwkv_w0"][i], p["rwkv_w_up"][i], p["rwkv_a0"][i], p["rwkv_a_up"][i],
                          p["rwkv_r_k"][i], p["rwkv_k_k"][i], p["rwkv_k_a"][i], p["rwkv_g_up"][i],
                          p["rwkv_lnx_g"][i], p["rwkv_lnx_b"][i])
        y_c = neighbourhood_attention(zqkv.reshape(b, s, 3 * c_c), p["na_rpb"][i]).reshape(t, c_c)

        merged = matmul(y_a, p["w_branch_a"][i].astype(BF16), F32, epilogue=lambda acc, g: acc * g,
                        tiles=(gates[:, :d],), tm=512, name="branch_a")
        merged = matmul(y_b, p["w_branch_b"][i].astype(BF16), F32, epilogue=lambda acc, g, m: m + acc * g,
                        tiles=(gates[:, d:2 * d], merged), tm=512, name="branch_b")
        merged = matmul(y_c, p["w_branch_c"][i].astype(BF16), BF16, epilogue=lambda acc, g, m: m + acc * g,
                        tiles=(gates[:, 2 * d:], merged), tm=512, name="branch_c")
        x = matmul(merged, p["w_out"][i].astype(BF16), F32, epilogue=lambda acc, res: res + acc,
                   tiles=(x,), tm=512, name="out_proj")
        xn = rmsnorm(x, p["norm2_g"][i], BF16)
        x = x + peer_ffn(xn, p["peer_wq"][i], p["peer_subkeys"][i], p["peer_u"][i], p["peer_v"][i])
    return rmsnorm(x, p["final_g"], F32).reshape(b, s, d)


def kernel(x_prompt, x_sample, norm1_g, w_in, rwkv_mu, rwkv_w0, rwkv_w_up, rwkv_a0, rwkv_a_up, rwkv_r_k, rwkv_k_k, rwkv_k_a, rwkv_g_up, rwkv_lnx_g, rwkv_lnx_b, na_rpb, w_branch_a, w_branch_b, w_branch_c, w_out, norm2_g, peer_wq, peer_subkeys, peer_u, peer_v, final_g):
    p = dict(norm1_g=norm1_g, w_in=w_in, rwkv_mu=rwkv_mu, rwkv_w0=rwkv_w0, rwkv_w_up=rwkv_w_up, rwkv_a0=rwkv_a0,
             rwkv_a_up=rwkv_a_up, rwkv_r_k=rwkv_r_k, rwkv_k_k=rwkv_k_k, rwkv_k_a=rwkv_k_a, rwkv_g_up=rwkv_g_up,
             rwkv_lnx_g=rwkv_lnx_g, rwkv_lnx_b=rwkv_lnx_b, na_rpb=na_rpb, w_branch_a=w_branch_a,
             w_branch_b=w_branch_b, w_branch_c=w_branch_c, w_out=w_out, norm2_g=norm2_g, peer_wq=peer_wq,
             peer_subkeys=peer_subkeys, peer_u=peer_u, peer_v=peer_v, final_g=final_g)
    nb = x_prompt.shape[0]
    if x_prompt.shape[1:] == x_sample.shape[1:]:
        y = encoder_trunk(jnp.concatenate([x_prompt, x_sample], axis=0), p)
        return (y[:nb], y[nb:])
    return (encoder_trunk(x_prompt, p), encoder_trunk(x_sample, p))
```
